```python
import jax, jax.numpy as jnp
from jax import lax
import numpy as np

D_MODEL = 2048
BATCH = 2
SEQ = 8192
DEPTH = 1
DEC_BATCH = 4
DEC_SEQ = 2048
PAST_LEN = 128

HEAD_DIM = 128
N_ATTN_HEADS = 8
N_KV_HEADS = 2
ATTN_GROUP = N_ATTN_HEADS // N_KV_HEADS
WINDOW = 128
BLOCK = 128
ROPE_THETA = 500000.0
ROT_DIM = HEAD_DIM // 4
N_FOURIER_GROUPS = 4
FOURIER_GROUP_DIM = 128
N_MEM_HEADS = 4
MEM_TOKENS = 256
ATTN_WIDTH = N_ATTN_HEADS * HEAD_DIM
KV_WIDTH = N_KV_HEADS * HEAD_DIM
FOURIER_WIDTH = N_FOURIER_GROUPS * FOURIER_GROUP_DIM
MEM_WIDTH = N_MEM_HEADS * HEAD_DIM
IN_WIDTH = ATTN_WIDTH + 2 * KV_WIDTH + FOURIER_WIDTH + MEM_WIDTH
MIX_WIDTH = ATTN_WIDTH + FOURIER_WIDTH + MEM_WIDTH
PEER_HEADS = 8
PEER_KEYS = 128
PEER_EXPERTS = PEER_KEYS * PEER_KEYS
PEER_QDIM = 256
PEER_HALF = PEER_QDIM // 2
PEER_TOPK = 16
PEER_CHUNK = 128
EPS = 1e-6
NEG = -1e30

kernel_name = 'hymba_fnet_peer_encoder'


def rmsnorm(x, g):
    xf = x.astype(jnp.float32)
    y = xf * lax.rsqrt(jnp.mean(xf * xf, axis=-1, keepdims=True) + EPS)
    return (y * g.astype(jnp.float32)).astype(x.dtype)


def rope_partial(x, pos):
    inv = ROPE_THETA ** (-jnp.arange(0, ROT_DIM, 2, dtype=jnp.float32) / ROT_DIM)
    ang = pos.astype(jnp.float32)[:, None] * inv[None, :]
    cos = jnp.cos(ang)[None, :, None, :]
    sin = jnp.sin(ang)[None, :, None, :]
    xr = x[..., :ROT_DIM].astype(jnp.float32)
    x1, x2 = xr[..., :ROT_DIM // 2], xr[..., ROT_DIM // 2:]
    rot = jnp.concatenate([x1 * cos - x2 * sin, x2 * cos + x1 * sin], axis=-1).astype(x.dtype)
    return jnp.concatenate([rot, x[..., ROT_DIM:]], axis=-1)


def windowed_gqa(q, k, v, sink):
    B, S = q.shape[0], q.shape[1]
    nb = S // BLOCK
    qb = q.reshape(B, nb, BLOCK, N_KV_HEADS, ATTN_GROUP, HEAD_DIM)
    pad = ((0, 0), (BLOCK, BLOCK), (0, 0), (0, 0))
    kp, vp = jnp.pad(k, pad), jnp.pad(v, pad)

    def bands(t):
        return jnp.concatenate(
            [t[:, o * BLOCK:o * BLOCK + S].reshape(B, nb, BLOCK, N_KV_HEADS, HEAD_DIM) for o in range(3)], axis=2)

    kb, vb = bands(kp), bands(vp)
    scale = HEAD_DIM ** -0.5
    s = jnp.einsum('bnqhgd,bnkhd->bnhgqk', qb, kb, preferred_element_type=jnp.float32) * scale
    qi = jnp.arange(BLOCK)[:, None]
    kj = jnp.arange(3 * BLOCK)[None, :]
    near = jnp.abs(kj - BLOCK - qi) <= WINDOW
    kpos = jnp.arange(nb)[:, None] * BLOCK + jnp.arange(3 * BLOCK)[None, :] - BLOCK
    inside = (kpos >= 0) & (kpos < S)
    mask = near[None, :, :] & inside[:, None, :]
    s = jnp.where(mask[None, :, None, None], s, NEG)
    sk = sink.astype(jnp.float32).reshape(1, 1, N_KV_HEADS, ATTN_GROUP, 1, 1)
    m = jnp.maximum(jnp.max(s, axis=-1, keepdims=True), sk)
    p = jnp.exp(s - m)
    denom = jnp.sum(p, axis=-1, keepdims=True) + jnp.exp(sk - m)
    o = jnp.einsum('bnhgqk,bnkhd->bnqhgd', (p / denom).astype(v.dtype), vb)
    return o.reshape(B, S, ATTN_WIDTH)


def memory_attention(qm, km, vm):
    B, S = qm.shape[0], qm.shape[1]
    s = jnp.einsum('bshd,bmhd->bhsm', qm, km, preferred_element_type=jnp.float32) * (HEAD_DIM ** -0.5)
    p = jax.nn.softmax(s, axis=-1).astype(vm.dtype)
    o = jnp.einsum('bhsm,bmhd->bshd', p, vm)
    return o.reshape(B, S, MEM_WIDTH)


def fourier_mix(f):
    B, S = f.shape[0], f.shape[1]
    fr = f.astype(jnp.float32).reshape(B, S, N_FOURIER_GROUPS, FOURIER_GROUP_DIM)
    out = jnp.fft.fftn(fr, axes=(1, 3), norm='ortho').real
    return out.reshape(B, S, FOURIER_WIDTH).astype(f.dtype)


def peer_ffn(h, w_peer_q, peer_subkeys, peer_u, peer_v):
    B, S, D = h.shape
    T = B * S
    ht = h.reshape(T, D)
    qp = (ht @ w_peer_q).astype(jnp.float32).reshape(T, PEER_HEADS, 2, PEER_HALF)
    s_half = jnp.einsum('thcd,hckd->thck', qp, peer_subkeys.astype(jnp.float32))
    sv, si = lax.top_k(s_half, PEER_TOPK)
    cand = (sv[:, :, 0, :, None] + sv[:, :, 1, None, :]).reshape(T, PEER_HEADS, PEER_TOPK * PEER_TOPK)
    cidx = (si[:, :, 0, :, None] * PEER_KEYS + si[:, :, 1, None, :]).reshape(T, PEER_HEADS, PEER_TOPK * PEER_TOPK)
    top_s, top_j = lax.top_k(cand, PEER_TOPK)
    eidx = jnp.take_along_axis(cidx, top_j, axis=-1)
    gate = jax.nn.softmax(top_s, axis=-1)
    nc = T // PEER_CHUNK
    E = PEER_HEADS * PEER_TOPK
    xs = ht.reshape(nc, PEER_CHUNK, D)
    es = eidx.reshape(nc, PEER_CHUNK, E)
    gs = gate.reshape(nc, PEER_CHUNK, E).astype(h.dtype)

    def expert_chunk(args):
        xc, ec, gc = args
        u = jnp.take(peer_u, ec, axis=0)
        a = jax.nn.gelu(jnp.einsum('td,ted->te', xc, u))
        vv = jnp.take(peer_v, ec, axis=0)
        return jnp.einsum('te,ted->td', gc * a, vv)

    out = lax.map(expert_chunk, (xs, es, gs))
    return out.reshape(B, S, D)


def encoder_layer(x, mem, g_norm1, g_norm_mem, w_in, g_q_attn, g_k_attn, attn_sink, w_mem_kv,
                  g_q_mem, g_k_mem, g_out_attn, g_out_fourier, g_out_mem, w_out, g_norm2,
                  w_peer_q, peer_subkeys, peer_u, peer_v):
    B, S, _ = x.shape
    pos = jnp.arange(S)
    h = rmsnorm(x, g_norm1)
    proj = h @ w_in
    o1 = ATTN_WIDTH
    o2 = o1 + KV_WIDTH
    o3 = o2 + KV_WIDTH
    o4 = o3 + FOURIER_WIDTH
    q = proj[..., :o1].reshape(B, S, N_ATTN_HEADS, HEAD_DIM)
    k = proj[..., o1:o2].reshape(B, S, N_KV_HEADS, HEAD_DIM)
    v = proj[..., o2:o3].reshape(B, S, N_KV_HEADS, HEAD_DIM)
    f = proj[..., o3:o4]
    qm = proj[..., o4:].reshape(B, S, N_MEM_HEADS, HEAD_DIM)
    q = rope_partial(rmsnorm(q, g_q_attn), pos)
    k = rope_partial(rmsnorm(k, g_k_attn), pos)
    o_attn = windowed_gqa(q, k, v, attn_sink)
    o_four = fourier_mix(f)
    memn = rmsnorm(mem, g_norm_mem)
    kvm = memn @ w_mem_kv
    M = mem.shape[1]
    km = rmsnorm(kvm[..., :MEM_WIDTH].reshape(B, M, N_MEM_HEADS, HEAD_DIM), g_k_mem)
    vm = kvm[..., MEM_WIDTH:].reshape(B, M, N_MEM_HEADS, HEAD_DIM)
    o_mem = memory_attention(rmsnorm(qm, g_q_mem), km, vm)
    mix = jnp.concatenate([rmsnorm(o_attn, g_out_attn), rmsnorm(o_four, g_out_fourier),
                           rmsnorm(o_mem, g_out_mem)], axis=-1)
    x = x + mix @ w_out
    x = x + peer_ffn(rmsnorm(x, g_norm2), w_peer_q, peer_subkeys, peer_u, peer_v)
    return x


def setup_inputs(seed: int = 0) -> dict:
    key = jax.random.key(seed)
    ks = jax.random.split(key, 24)
    nrm = jax.random.normal
    f32 = jnp.float32

    def gain(k, n):
        return 1.0 + 0.02 * nrm(k, (DEPTH, n), f32)

    return {
        'x_prompt': nrm(ks[0], (BATCH, SEQ, D_MODEL), f32),
        'x_sample': nrm(ks[1], (DEC_BATCH, DEC_SEQ, D_MODEL), f32),
        'mem_prompt': nrm(ks[2], (BATCH, MEM_TOKENS, D_MODEL), f32),
        'mem_sample': nrm(ks[3], (DEC_BATCH, MEM_TOKENS, D_MODEL), f32),
        'g_norm1': gain(ks[4], D_MODEL),
        'g_norm_mem': gain(ks[5], D_MODEL),
        'w_in': nrm(ks[6], (DEPTH, D_MODEL, IN_WIDTH), f32) * D_MODEL ** -0.5,
        'g_q_attn': gain(ks[7], HEAD_DIM),
        'g_k_attn': gain(ks[8], HEAD_DIM),
        'attn_sink': 0.5 * nrm(ks[9], (DEPTH, N_ATTN_HEADS), f32),
        'w_mem_kv': nrm(ks[10], (DEPTH, D_MODEL, 2 * MEM_WIDTH), f32) * D_MODEL ** -0.5,
        'g_q_mem': gain(ks[11], HEAD_DIM),
        'g_k_mem': gain(ks[12], HEAD_DIM),
        'g_out_attn': gain(ks[13], ATTN_WIDTH),
        'g_out_fourier': gain(ks[14], FOURIER_WIDTH),
        'g_out_mem': gain(ks[15], MEM_WIDTH),
        'w_out': nrm(ks[16], (DEPTH, MIX_WIDTH, D_MODEL), f32) * MIX_WIDTH ** -0.5,
        'g_norm2': gain(ks[17], D_MODEL),
        'w_peer_q': nrm(ks[18], (DEPTH, D_MODEL, PEER_HEADS * PEER_QDIM), f32) * D_MODEL ** -0.5,
        'peer_subkeys': nrm(ks[19], (DEPTH, PEER_HEADS, 2, PEER_KEYS, PEER_HALF), f32) * PEER_HALF ** -0.5,
        'peer_u': nrm(ks[20], (DEPTH, PEER_EXPERTS, D_MODEL), f32) * D_MODEL ** -0.5,
        'peer_v': 0.5 * nrm(ks[21], (DEPTH, PEER_EXPERTS, D_MODEL), f32),
    }


def reference(x_prompt, x_sample, mem_prompt, mem_sample, g_norm1, g_norm_mem, w_in, g_q_attn,
              g_k_attn, attn_sink, w_mem_kv, g_q_mem, g_k_mem, g_out_attn, g_out_fourier, g_out_mem,
              w_out, g_norm2, w_peer_q, peer_subkeys, peer_u, peer_v):
    def layer(x, mem, l):
        return encoder_layer(x, mem, g_norm1[l], g_norm_mem[l], w_in[l], g_q_attn[l], g_k_attn[l],
                             attn_sink[l], w_mem_kv[l], g_q_mem[l], g_k_mem[l], g_out_attn[l],
                             g_out_fourier[l], g_out_mem[l], w_out[l], g_norm2[l], w_peer_q[l],
                             peer_subkeys[l], peer_u[l], peer_v[l])

    y_prompt = x_prompt
    y_sample = x_sample
    for l in range(DEPTH):
        y_prompt = layer(y_prompt, mem_prompt, l)
        y_sample = layer(y_sample, mem_sample, l)
    return (y_prompt, y_sample)
```

```python
import functools
import math

import numpy as np
import jax
import jax.numpy as jnp
from jax import lax
from jax.experimental import pallas as pl
from jax.experimental.pallas import tpu as pltpu

D_MODEL = 2048
HEAD_DIM = 128
N_ATTN_HEADS = 8
N_KV_HEADS = 2
ATTN_GROUP = N_ATTN_HEADS // N_KV_HEADS
BLOCK = 128
ROPE_THETA = 500000.0
ROT_DIM = HEAD_DIM // 4
N_FOURIER_GROUPS = 4
FOURIER_GROUP_DIM = 128
N_MEM_HEADS = 4
ATTN_WIDTH = N_ATTN_HEADS * HEAD_DIM
KV_WIDTH = N_KV_HEADS * HEAD_DIM
FOURIER_WIDTH = N_FOURIER_GROUPS * FOURIER_GROUP_DIM
MEM_WIDTH = N_MEM_HEADS * HEAD_DIM
IN_WIDTH = ATTN_WIDTH + 2 * KV_WIDTH + FOURIER_WIDTH + MEM_WIDTH
MIX_WIDTH = ATTN_WIDTH + FOURIER_WIDTH + MEM_WIDTH
PEER_HEADS = 8
PEER_KEYS = 128
PEER_EXPERTS = PEER_KEYS * PEER_KEYS
PEER_HALF = 128
PEER_TOPK = 16
EPS = 1e-6
NEG = -1e30

F32 = jnp.float32
BF16 = jnp.bfloat16
VMEM_LIMIT = 56 * 1024 * 1024

_NT = (((1,), (1,)), ((), ()))
_TN = (((0,), (0,)), ((), ()))


def _params(*sem):
    return pltpu.CompilerParams(dimension_semantics=sem, vmem_limit_bytes=VMEM_LIMIT)


def _const_spec(shape):
    return pl.BlockSpec(shape, lambda *_: (0,) * len(shape))


def _rms(x, g, width):
    ms = jnp.sum(x * x, axis=-1, keepdims=True) * (1.0 / width)
    return x * lax.rsqrt(ms + EPS) * g


def _inproj_body(x_ref, g1_ref, w_ref, gq_ref, gk_ref, gqm_ref, rc_ref, rs1_ref, rs2_ref, dft_ref,
                 q_ref, k_ref, v_ref, qm_ref, yr_ref, yi_ref):
    x = x_ref[...]
    h = _rms(x, g1_ref[...], D_MODEL).astype(BF16)
    proj = jnp.dot(h, w_ref[...], preferred_element_type=F32)
    rc, rs1, rs2 = rc_ref[...], rs1_ref[...], rs2_ref[...]
    scale = HEAD_DIM ** -0.5

    def rope(t):
        return t * rc + pltpu.roll(t, HEAD_DIM - ROT_DIM // 2, 1) * rs1 + pltpu.roll(t, ROT_DIM // 2, 1) * rs2

    for hd in range(N_ATTN_HEADS):
        t = proj[:, hd * HEAD_DIM:(hd + 1) * HEAD_DIM]
        t = rope(_rms(t, gq_ref[...], HEAD_DIM)) * scale
        q_ref[:, hd * HEAD_DIM:(hd + 1) * HEAD_DIM] = t.astype(BF16)
    o1 = ATTN_WIDTH
    for hd in range(N_KV_HEADS):
        t = proj[:, o1 + hd * HEAD_DIM:o1 + (hd + 1) * HEAD_DIM]
        t = rope(_rms(t, gk_ref[...], HEAD_DIM))
        k_ref[:, hd * HEAD_DIM:(hd + 1) * HEAD_DIM] = t.astype(BF16)
    o2 = o1 + KV_WIDTH
    v_ref[...] = proj[:, o2:o2 + KV_WIDTH].astype(BF16)
    o3 = o2 + KV_WIDTH
    for g in range(N_FOURIER_GROUPS):
        f = proj[:, o3 + g * FOURIER_GROUP_DIM:o3 + (g + 1) * FOURIER_GROUP_DIM]
        y = jnp.dot(f, dft_ref[...], preferred_element_type=F32, precision=lax.Precision.HIGHEST)
        yr_ref[:, g * FOURIER_GROUP_DIM:(g + 1) * FOURIER_GROUP_DIM] = y[:, :FOURIER_GROUP_DIM]
        yi_ref[:, g * FOURIER_GROUP_DIM:(g + 1) * FOURIER_GROUP_DIM] = y[:, FOURIER_GROUP_DIM:]
    o4 = o3 + FOURIER_WIDTH
    for hd in range(N_MEM_HEADS):
        t = proj[:, o4 + hd * HEAD_DIM:o4 + (hd + 1) * HEAD_DIM]
        t = _rms(t, gqm_ref[...], HEAD_DIM) * scale
        qm_ref[:, hd * HEAD_DIM:(hd + 1) * HEAD_DIM] = t.astype(BF16)


def _inproj(x2, seq, g1, w_in, gq, gk, gqm, rope_tabs, dft, tm):
    T = x2.shape[0]
    nseq = seq // tm
    tok = lambda w: pl.BlockSpec((tm, w), lambda i: (i, 0))
    rope_spec = pl.BlockSpec((tm, HEAD_DIM), lambda i: (i % nseq, 0))
    outs = [(ATTN_WIDTH, BF16), (KV_WIDTH, BF16), (KV_WIDTH, BF16), (MEM_WIDTH, BF16),
            (FOURIER_WIDTH, F32), (FOURIER_WIDTH, F32)]
    return pl.pallas_call(
        _inproj_body,
        grid=(T // tm,),
        in_specs=[tok(D_MODEL), _const_spec((1, D_MODEL)), _const_spec((D_MODEL, IN_WIDTH)),
                  _const_spec((1, HEAD_DIM)), _const_spec((1, HEAD_DIM)), _const_spec((1, HEAD_DIM)),
                  rope_spec, rope_spec, rope_spec, _const_spec((FOURIER_GROUP_DIM, 2 * FOURIER_GROUP_DIM))],
        out_specs=[tok(w) for w, _ in outs],
        out_shape=[jax.ShapeDtypeStruct((T, w), dt) for w, dt in outs],
        compiler_params=_params("parallel"),
        name="inproj",
    )(x2, g1, w_in, gq, gk, gqm, *rope_tabs, dft)


def _memkv_body(mem_ref, g_ref, w_ref, gk_ref, km_ref, vm_ref):
    h = _rms(mem_ref[...], g_ref[...], D_MODEL).astype(BF16)
    kv = jnp.dot(h, w_ref[...], preferred_element_type=F32)
    for hd in range(N_MEM_HEADS):
        t = kv[:, hd * HEAD_DIM:(hd + 1) * HEAD_DIM]
        km_ref[:, hd * HEAD_DIM:(hd + 1) * HEAD_DIM] = _rms(t, gk_ref[...], HEAD_DIM).astype(BF16)
    vm_ref[...] = kv[:, MEM_WIDTH:].astype(BF16)


def _memkv(mem2, m_tokens, g, w, gk):
    R = mem2.shape[0]
    blk = lambda w_: pl.BlockSpec((m_tokens, w_), lambda i: (i, 0))
    return pl.pallas_call(
        _memkv_body,
        grid=(R // m_tokens,),
        in_specs=[blk(D_MODEL), _const_spec((1, D_MODEL)), _const_spec((D_MODEL, 2 * MEM_WIDTH)),
                  _const_spec((1, HEAD_DIM))],
        out_specs=[blk(MEM_WIDTH), blk(MEM_WIDTH)],
        out_shape=[jax.ShapeDtypeStruct((R, MEM_WIDTH), BF16)] * 2,
        compiler_params=_params("parallel"),
        name="memkv",
    )(mem2, g, w, gk)


def _attn_body(sink_ref, q_ref, kp_ref, ko_ref, kn_ref, vp_ref, vo_ref, vn_ref, qm_ref, km_ref, vm_ref,
               ga_ref, gc_ref, oa_ref, oc_ref, *, nb):
    n = pl.program_id(1)
    rows = ATTN_GROUP * BLOCK
    qi = lax.broadcasted_iota(jnp.int32, (rows, 3 * BLOCK), 0) & (BLOCK - 1)
    kj = lax.broadcasted_iota(jnp.int32, (rows, 3 * BLOCK), 1)
    lo = jnp.where(n > 0, 0, BLOCK)
    hi = jnp.where(n < nb - 1, 3 * BLOCK, 2 * BLOCK)
    mask = (jnp.abs(kj - BLOCK - qi) <= BLOCK) & (kj >= lo) & (kj < hi)
    sink_row = lax.broadcasted_iota(jnp.int32, (rows, 1), 0) // BLOCK

    outs = []
    ssq = jnp.zeros((BLOCK, 1), F32)
    for h in range(N_KV_HEADS):
        cs = slice(h * HEAD_DIM, (h + 1) * HEAD_DIM)
        kb = jnp.concatenate([kp_ref[:, cs], ko_ref[:, cs], kn_ref[:, cs]], axis=0)
        vb = jnp.concatenate([vp_ref[:, cs], vo_ref[:, cs], vn_ref[:, cs]], axis=0)
        qs = jnp.concatenate(
            [q_ref[:, (h * ATTN_GROUP + g) * HEAD_DIM:(h * ATTN_GROUP + g + 1) * HEAD_DIM]
             for g in range(ATTN_GROUP)], axis=0)
        s = lax.dot_general(qs, kb, _NT, preferred_element_type=F32)
        s = jnp.where(mask, s, NEG)
        sk = jnp.zeros((rows, 1), F32)
        for g in range(ATTN_GROUP):
            sk = jnp.where(sink_row == g, sink_ref[h * ATTN_GROUP + g], sk)
        m = jnp.maximum(jnp.max(s, axis=-1, keepdims=True), sk)
        p = jnp.exp(s - m)
        denom = jnp.sum(p, axis=-1, keepdims=True) + jnp.exp(sk - m)
        pn = (p / denom).astype(BF16)
        o = jnp.dot(pn, vb, preferred_element_type=F32)
        for g in range(ATTN_GROUP):
            og = o[g * BLOCK:(g + 1) * BLOCK]
            ssq = ssq + jnp.sum(og * og, axis=-1, keepdims=True)
            outs.append(og)
    inv = lax.rsqrt(ssq * (1.0 / ATTN_WIDTH) + EPS)
    for hd in range(N_ATTN_HEADS):
        cs = slice(hd * HEAD_DIM, (hd + 1) * HEAD_DIM)
        oa_ref[:, cs] = (outs[hd] * inv * ga_ref[:, cs]).astype(BF16)

    mouts = []
    ssq = jnp.zeros((BLOCK, 1), F32)
    for hd in range(N_MEM_HEADS):
        cs = slice(hd * HEAD_DIM, (hd + 1) * HEAD_DIM)
        s = lax.dot_general(qm_ref[:, cs], km_ref[:, cs], _NT, preferred_element_type=F32)
        m = jnp.max(s, axis=-1, keepdims=True)
        p = jnp.exp(s - m)
        pn = (p / jnp.sum(p, axis=-1, keepdims=True)).astype(BF16)
        o = jnp.dot(pn, vm_ref[:, cs], preferred_element_type=F32)
        ssq = ssq + jnp.sum(o * o, axis=-1, keepdims=True)
        mouts.append(o)
    inv = lax.rsqrt(ssq * (1.0 / MEM_WIDTH) + EPS)
    for hd in range(N_MEM_HEADS):
        cs = slice(hd * HEAD_DIM, (hd + 1) * HEAD_DIM)
        oc_ref[:, cs] = (mouts[hd] * inv * gc_ref[:, cs]).astype(BF16)


def _attention(sink, q, k, v, qm, km, vm, ga, gc, batch, seq, m_tokens):
    nb = seq // BLOCK
    T = batch * seq
    own = lambda w: pl.BlockSpec((BLOCK, w), lambda b, n: (b * nb + n, 0))
    prev = lambda w: pl.BlockSpec((BLOCK, w), lambda b, n: (b * nb + jnp.maximum(n - 1, 0), 0))
    nxt = lambda w: pl.BlockSpec((BLOCK, w), lambda b, n: (b * nb + jnp.minimum(n + 1, nb - 1), 0))
    memb = pl.BlockSpec((m_tokens, MEM_WIDTH), lambda b, n: (b, 0))
    return pl.pallas_call(
        functools.partial(_attn_body, nb=nb),
        grid=(batch, nb),
        in_specs=[pl.BlockSpec(memory_space=pltpu.SMEM),
                  own(ATTN_WIDTH), prev(KV_WIDTH), own(KV_WIDTH), nxt(KV_WIDTH),
                  prev(KV_WIDTH), own(KV_WIDTH), nxt(KV_WIDTH), own(MEM_WIDTH), memb, memb,
                  _const_spec((1, ATTN_WIDTH)), _const_spec((1, MEM_WIDTH))],
        out_specs=[own(ATTN_WIDTH), own(MEM_WIDTH)],
        out_shape=[jax.ShapeDtypeStruct((T, ATTN_WIDTH), BF16), jax.ShapeDtypeStruct((T, MEM_WIDTH), BF16)],
        compiler_params=_params("parallel", "parallel"),
        name="attention",
    )(sink, q, k, k, k, v, v, v, qm, km, vm, ga, gc)


def _fft_a_body(yr_ref, yi_ref, m_ref, tc_ref, ts_ref, br_ref, bi_ref, *, n1):
    y = jnp.concatenate([yr_ref[...], yi_ref[...]], axis=0)
    a = jnp.dot(m_ref[...], y, preferred_element_type=F32, precision=lax.Precision.HIGHEST)
    ar, ai = a[:n1], a[n1:]
    tc, ts = tc_ref[...], ts_ref[...]
    br_ref[...] = ar * tc + ai * ts
    bi_ref[...] = ai * tc - ar * ts


def _fft_c_body(br_ref, bi_ref, f_ref, g_ref, o_ref, *, k1b, scale):
    for j in range(k1b):
        b = jnp.concatenate([br_ref[j], bi_ref[j]], axis=0)
        x = jnp.dot(f_ref[...], b, preferred_element_type=F32, precision=lax.Precision.HIGHEST) * scale
        o_ref[:, j * FOURIER_WIDTH:(j + 1) * FOURIER_WIDTH] = _rms(x, g_ref[...], FOURIER_WIDTH).astype(BF16)


def _dft_consts(seq):
    n1, n2 = seq // 128, 128
    a1 = 2.0 * np.pi * np.outer(np.arange(n1), np.arange(n1)) / n1
    c1, s1 = np.cos(a1), np.sin(a1)
    m1 = np.block([[c1, s1], [-s1, c1]]).astype(np.float32)
    at = 2.0 * np.pi * np.outer(np.arange(n1), np.arange(n2)) / seq
    tc = np.repeat(np.cos(at), FOURIER_WIDTH, axis=1).astype(np.float32)
    ts = np.repeat(np.sin(at), FOURIER_WIDTH, axis=1).astype(np.float32)
    a2 = 2.0 * np.pi * np.outer(np.arange(n2), np.arange(n2)) / n2
    f2 = np.concatenate([np.cos(a2), np.sin(a2)], axis=1).astype(np.float32)
    return m1, tc, ts, f2


def _fourier(yr, yi, g_four, batch, seq, cb, k1b):
    n1, n2, C = seq // 128, 128, FOURIER_WIDTH
    m1, tc, ts, f2 = _dft_consts(seq)
    yr3 = yr.reshape(batch, n1, n2 * C)
    yi3 = yi.reshape(batch, n1, n2 * C)
    data = pl.BlockSpec((None, n1, cb), lambda b, c: (b, 0, c))
    tw = pl.BlockSpec((n1, cb), lambda b, c: (0, c))
    br, bi = pl.pallas_call(
        functools.partial(_fft_a_body, n1=n1),
        grid=(batch, n2 * C // cb),
        in_specs=[data, data, _const_spec((2 * n1, 2 * n1)), tw, tw],
        out_specs=[data, data],
        out_shape=[jax.ShapeDtypeStruct((batch, n1, n2 * C), F32)] * 2,
        compiler_params=_params("parallel", "parallel"),
        name="fft_a",
    )(yr3, yi3, m1, tc, ts)
    br4 = br.reshape(batch, n1, n2, C)
    bi4 = bi.reshape(batch, n1, n2, C)
    bspec = pl.BlockSpec((None, k1b, n2, C), lambda b, k: (b, k, 0, 0))
    out = pl.pallas_call(
        functools.partial(_fft_c_body, k1b=k1b, scale=1.0 / math.sqrt(seq * FOURIER_GROUP_DIM)),
        grid=(batch, n1 // k1b),
        in_specs=[bspec, bspec, _const_spec((n2, 2 * n2)), _const_spec((1, C))],
        out_specs=pl.BlockSpec((None, n2, k1b * C), lambda b, k: (b, 0, k)),
        out_shape=jax.ShapeDtypeStruct((batch, n2, n1 * C), BF16),
        compiler_params=_params("parallel", "parallel"),
        name="fft_c",
    )(br4, bi4, f2, g_four)
    return out.reshape(batch * seq, C)


def _outproj_body(x_ref, a_ref, b_ref, c_ref, w_ref, o_ref):
    acc = jnp.dot(a_ref[...], w_ref[:ATTN_WIDTH], preferred_element_type=F32)
    acc += jnp.dot(b_ref[...], w_ref[ATTN_WIDTH:ATTN_WIDTH + FOURIER_WIDTH], preferred_element_type=F32)
    acc += jnp.dot(c_ref[...], w_ref[ATTN_WIDTH + FOURIER_WIDTH:], preferred_element_type=F32)
    o_ref[...] = x_ref[...] + acc


def _outproj(x2, mix_a, mix_b, mix_c, w_out, tm):
    T = x2.shape[0]
    tok = lambda w: pl.BlockSpec((tm, w), lambda i: (i, 0))
    return pl.pallas_call(
        _outproj_body,
        grid=(T // tm,),
        in_specs=[tok(D_MODEL), tok(ATTN_WIDTH), tok(FOURIER_WIDTH), tok(MEM_WIDTH),
                  _const_spec((MIX_WIDTH, D_MODEL))],
        out_specs=tok(D_MODEL),
        out_shape=jax.ShapeDtypeStruct((T, D_MODEL), F32),
        compiler_params=_params("parallel"),
        name="outproj",
    )(x2, mix_a, mix_b, mix_c, w_out)


def _top16(s):
    vals = []
    for _ in range(PEER_TOPK):
        m = jnp.max(s, axis=0, keepdims=True)
        vals.append(m)
        s = jnp.where(s == m, -jnp.inf, s)
    return jnp.concatenate(vals, axis=0)


_CAND_ROWS = (5, 4, 3, 2, 2, 2)


def _route_body(x_ref, g_ref, wq_ref, sub_ref, hn_ref, s0_ref, s1_ref, e0_ref, e1_ref, tau_ref, qp_ref):
    hn = _rms(x_ref[...], g_ref[...], D_MODEL).astype(BF16)
    hn_ref[...] = hn
    qp_ref[...] = lax.dot_general(wq_ref[...], hn, _NT, preferred_element_type=F32).astype(BF16)
    tt = x_ref.shape[0]
    row8 = lax.broadcasted_iota(jnp.int32, (8, tt), 0)

    def head(h, carry):
        base = pl.multiple_of(h * 2 * PEER_HALF, 2 * PEER_HALF)
        s0 = jnp.dot(sub_ref[2 * h], qp_ref[pl.ds(base, PEER_HALF), :], preferred_element_type=F32)
        s1 = jnp.dot(sub_ref[2 * h + 1], qp_ref[pl.ds(base + PEER_HALF, PEER_HALF), :],
                     preferred_element_type=F32)
        a0 = _top16(s0)
        a1 = _top16(s1)
        groups = [a0[0:1] + a1, a0[1:2] + a1[0:8]]
        for p, nq in zip(range(2, 8), _CAND_ROWS):
            groups.append(jnp.where(row8 < nq, a0[p:p + 1] + a1[0:8], -jnp.inf))
        groups.append(a0[8:16] + a1[0:1])
        top = _top16(jnp.concatenate(groups, axis=0))
        mx = top[0:1]
        z = jnp.sum(jnp.exp(top - mx), axis=0, keepdims=True)
        s0_ref[h] = s0
        s1_ref[h] = s1
        e0_ref[h] = jnp.exp(s0 - a0[0:1])
        e1_ref[h] = jnp.exp(s1 - a1[0:1]) / z
        tau_ref[pl.ds(h, 1), :] = top[PEER_TOPK - 1:PEER_TOPK]
        return carry

    lax.fori_loop(0, PEER_HEADS, head, 0)


def _route(x1, g2, wq_t, sub, tt):
    T = x1.shape[0]
    hk = pl.BlockSpec((PEER_HEADS, PEER_KEYS, tt), lambda i: (0, 0, i))
    hk_shape = jax.ShapeDtypeStruct((PEER_HEADS, PEER_KEYS, T), F32)
    return pl.pallas_call(
        _route_body,
        grid=(T // tt,),
        in_specs=[pl.BlockSpec((tt, D_MODEL), lambda i: (i, 0)), _const_spec((1, D_MODEL)),
                  _const_spec((D_MODEL, D_MODEL)), _const_spec((2 * PEER_HEADS, PEER_KEYS, PEER_HALF))],
        out_specs=[pl.BlockSpec((tt, D_MODEL), lambda i: (i, 0)), hk, hk, hk, hk,
                   pl.BlockSpec((PEER_HEADS, tt), lambda i: (0, i))],
        out_shape=[jax.ShapeDtypeStruct((T, D_MODEL), BF16), hk_shape, hk_shape, hk_shape, hk_shape,
                   jax.ShapeDtypeStruct((PEER_HEADS, T), F32)],
        scratch_shapes=[pltpu.VMEM((D_MODEL, tt), BF16)],
        compiler_params=_params("parallel"),
        name="peer_route",
    )(x1, g2, wq_t, sub)


def _gelu(x):
    return 0.5 * x * (1.0 + jnp.tanh(math.sqrt(2.0 / math.pi) * (x + 0.044715 * (x * x * x))))


def _expert_body(hn_ref, u_ref, v_ref, s0_ref, s1_ref, e0_ref, e1_ref, tau_ref, x1_ref, y_ref,
                 at_ref, wt_ref, acc_ref, *, rows):
    e = pl.program_id(1)

    @pl.when(e == 0)
    def _():
        acc_ref[...] = jnp.zeros_like(acc_ref)

    at_ref[...] = lax.dot_general(u_ref[...], hn_ref[...], _NT, preferred_element_type=F32)

    def key_row(r, carry):
        i = e * rows + r
        off = pl.multiple_of(r * PEER_KEYS, PEER_KEYS)
        g = jnp.zeros((PEER_KEYS, hn_ref.shape[0]), F32)
        for h in range(PEER_HEADS):
            sm = s0_ref[h, pl.ds(i, 1), :] + s1_ref[h]
            w = e0_ref[h, pl.ds(i, 1), :] * e1_ref[h]
            g = g + jnp.where(sm >= tau_ref[h:h + 1, :], w, 0.0)
        wt_ref[pl.ds(off, PEER_KEYS), :] = (g * _gelu(at_ref[pl.ds(off, PEER_KEYS), :])).astype(BF16)
        return carry

    lax.fori_loop(0, rows, key_row, 0)
    acc_ref[...] += lax.dot_general(wt_ref[...], v_ref[...], _TN, preferred_element_type=F32)

    @pl.when(e == pl.num_programs(1) - 1)
    def _():
        y_ref[...] = x1_ref[...] + acc_ref[...]


def _experts(hn, u, v, s0, s1, e0, e1, tau, x1, tt, eb):
    T = hn.shape[0]
    tok = pl.BlockSpec((tt, D_MODEL), lambda t, e: (t, 0))
    tok1 = pl.BlockSpec((tt, D_MODEL), lambda t, e: (t, 0), pipeline_mode=pl.Buffered(1))
    wgt = pl.BlockSpec((eb, D_MODEL), lambda t, e: (e, 0))
    hk = pl.BlockSpec((PEER_HEADS, PEER_KEYS, tt), lambda t, e: (0, 0, t), pipeline_mode=pl.Buffered(1))
    return pl.pallas_call(
        functools.partial(_expert_body, rows=eb // PEER_KEYS),
        grid=(T // tt, PEER_EXPERTS // eb),
        in_specs=[tok1, wgt, wgt, hk, hk, hk, hk, pl.BlockSpec((PEER_HEADS, tt), lambda t, e: (0, t)), tok1],
        out_specs=tok,
        out_shape=jax.ShapeDtypeStruct((T, D_MODEL), F32),
        scratch_shapes=[pltpu.VMEM((eb, tt), F32), pltpu.VMEM((eb, tt), BF16), pltpu.VMEM((tt, D_MODEL), F32)],
        compiler_params=_params("parallel", "arbitrary"),
        name="peer_experts",
    )(hn, u, v, s0, s1, e0, e1, tau, x1)


def _rope_tables(seq):
    inv = ROPE_THETA ** (-np.arange(0, ROT_DIM, 2, dtype=np.float64) / ROT_DIM)
    ang = np.arange(seq, dtype=np.float64)[:, None] * inv[None, :]
    half = ROT_DIM // 2
    c = np.ones((seq, HEAD_DIM))
    s1 = np.zeros((seq, HEAD_DIM))
    s2 = np.zeros((seq, HEAD_DIM))
    c[:, :half] = np.cos(ang)
    c[:, half:ROT_DIM] = np.cos(ang)
    s1[:, :half] = -np.sin(ang)
    s2[:, half:ROT_DIM] = np.sin(ang)
    return tuple(jnp.asarray(t, F32) for t in (c, s1, s2))


def _channel_dft():
    a = 2.0 * np.pi * np.outer(np.arange(FOURIER_GROUP_DIM), np.arange(FOURIER_GROUP_DIM)) / FOURIER_GROUP_DIM
    return jnp.asarray(np.concatenate([np.cos(a), -np.sin(a)], axis=1), F32)


def _layer(x, mem, w, *, tm, cb, k1b, tr, tt, eb):
    batch, seq, _ = x.shape
    m_tokens = mem.shape[1]
    x2 = x.reshape(batch * seq, D_MODEL)
    q, k, v, qm, yr, yi = _inproj(x2, seq, w["g_norm1"], w["w_in"], w["g_q_attn"], w["g_k_attn"], w["g_q_mem"],
                                  _rope_tables(seq), _channel_dft(), tm)
    km, vm = _memkv(mem.reshape(batch * m_tokens, D_MODEL), m_tokens, w["g_norm_mem"], w["w_mem_kv"],
                    w["g_k_mem"])
    mix_a, mix_c = _attention(w["attn_sink"], q, k, v, qm, km, vm, w["g_out_attn"], w["g_out_mem"],
                              batch, seq, m_tokens)
    mix_b = _fourier(yr, yi, w["g_out_fourier"], batch, seq, cb, k1b)
    x1 = _outproj(x2, mix_a, mix_b, mix_c, w["w_out"], tm)
    hn, s0, s1, e0, e1, tau = _route(x1, w["g_norm2"], w["w_peer_q_t"], w["peer_subkeys"], tr)
    y = _experts(hn, w["peer_u"], w["peer_v"], s0, s1, e0, e1, tau, x1, tt, eb)
    return y.reshape(batch, seq, D_MODEL)


def _prep_weights(g_norm1, g_norm_mem, w_in, g_q_attn, g_k_attn, attn_sink, w_mem_kv, g_q_mem, g_k_mem,
                  g_out_attn, g_out_fourier, g_out_mem, w_out, g_norm2, w_peer_q, peer_subkeys, peer_u, peer_v):
    row = lambda g: g.reshape(1, -1).astype(F32)
    return dict(
        g_norm1=row(g_norm1), g_norm_mem=row(g_norm_mem), w_in=w_in.astype(BF16),
        g_q_attn=row(g_q_attn), g_k_attn=row(g_k_attn), attn_sink=attn_sink.astype(F32),
        w_mem_kv=w_mem_kv.astype(BF16), g_q_mem=row(g_q_mem), g_k_mem=row(g_k_mem),
        g_out_attn=row(g_out_attn), g_out_fourier=row(g_out_fourier), g_out_mem=row(g_out_mem),
        w_out=w_out.astype(BF16), g_norm2=row(g_norm2), w_peer_q_t=w_peer_q.T.astype(BF16),
        peer_subkeys=peer_subkeys.reshape(2 * PEER_HEADS, PEER_KEYS, PEER_HALF).astype(BF16),
        peer_u=peer_u.astype(BF16), peer_v=peer_v.astype(BF16))


def kernel(x_prompt, x_sample, mem_prompt, mem_sample, g_norm1, g_norm_mem, w_in, g_q_attn, g_k_attn, attn_sink,
           w_mem_kv, g_q_mem, g_k_mem, g_out_attn, g_out_fourier, g_out_mem, w_out, g_norm2, w_peer_q,
           peer_subkeys, peer_u, peer_v):
    stacked = (g_norm1, g_norm_mem, w_in, g_q_attn, g_k_attn, attn_sink, w_mem_kv, g_q_mem, g_k_mem, g_out_attn,
               g_out_fourier, g_out_mem, w_out, g_norm2, w_peer_q, peer_subkeys, peer_u, peer_v)
    y_prompt, y_sample = x_prompt, x_sample
    for l in range(g_norm1.shape[0]):
        w = _prep_weights(*(p[l] for p in stacked))
        tiles = dict(tm=256, cb=4096, k1b=4, tr=256, tt=512, eb=1024)
        y_prompt = _layer(y_prompt, mem_prompt, w, **tiles)
        y_sample = _layer(y_sample, mem_sample, w, **tiles)
    return (y_prompt, y_sample)
```
